```python
import jax, jax.numpy as jnp
from jax import lax
import numpy as np

D_MODEL = 2048
BATCH = 4
SEQ = 4096
DEPTH = 1
DEC_BATCH = 32
DEC_SEQ = 1
PAST_LEN = 16384
PAGE_SIZE = 128

HEAD_DIM = 128
N_HEADS_A = D_MODEL // 256
N_KV_A = 2
GROUP_R = N_HEADS_A // N_KV_A
CMP_STRIDE = 16
CMP_LEN = 2 * CMP_STRIDE
CMP_HIDDEN = HEAD_DIM
SEL_LEN = 64
N_SEL = 16
WINDOW = 512
Q_BLOCK = 128
N_HEADS_R = D_MODEL // 256
DK_R = 128
DV_R = 2 * DK_R
RET_CHUNK = 128
ROPE_THETA = 10000.0
EPS = 1e-6
NEG = -1e30
BIG = 1e30

Q_A_W = N_HEADS_A * HEAD_DIM
KV_A_W = N_KV_A * HEAD_DIM
GATE_A_W = 3 * N_HEADS_A
Q_R_W = N_HEADS_R * DK_R
V_R_W = N_HEADS_R * DV_R
MERGE_W = 2 * D_MODEL
SPLIT_WIDTHS = (Q_A_W, KV_A_W, KV_A_W, KV_A_W, KV_A_W, KV_A_W, KV_A_W, GATE_A_W, Q_A_W,
                Q_R_W, Q_R_W, V_R_W, V_R_W, MERGE_W)
D_IN = sum(SPLIT_WIDTHS)

kernel_name = "hybrid_nsa_retention_decode_step"


def rmsnorm(x, g):
    xf = x.astype(jnp.float32)
    y = xf * lax.rsqrt(jnp.mean(xf * xf, axis=-1, keepdims=True) + EPS)
    return (y * g.astype(jnp.float32)).astype(x.dtype)


def rope(x, pos):
    half = x.shape[-1] // 2
    inv = ROPE_THETA ** (-jnp.arange(half, dtype=jnp.float32) / half)
    ang = pos.astype(jnp.float32)[:, None] * inv[None, :]
    cos = jnp.cos(ang)[:, None, :]
    sin = jnp.sin(ang)[:, None, :]
    xf = x.astype(jnp.float32)
    x1, x2 = xf[..., :half], xf[..., half:]
    return jnp.concatenate([x1 * cos - x2 * sin, x2 * cos + x1 * sin], axis=-1).astype(x.dtype)


def masked_softmax(s, mask):
    s = jnp.where(mask, s, NEG)
    m = jnp.max(s, axis=-1, keepdims=True)
    p = jnp.exp(s - m) * mask
    return p / jnp.maximum(p.sum(-1, keepdims=True), 1e-30)


def attend_shared(q, k, v, mask):
    B, Q, H, d = q.shape
    qg = q.reshape(B, Q, N_KV_A, GROUP_R, d).astype(jnp.float32)
    s = jnp.einsum('bqgrd,bngd->bgrqn', qg, k.astype(jnp.float32)) * d ** -0.5
    p = masked_softmax(s, mask)
    o = jnp.einsum('bgrqn,bngd->bqgrd', p, v.astype(jnp.float32))
    return o.reshape(B, Q, H, d), p


def attend_gathered(q, k, v, mask):
    B, Q, H, d = q.shape
    qg = q.reshape(B, Q, N_KV_A, GROUP_R, d).astype(jnp.float32)
    s = jnp.einsum('bqgrd,bqgnd->bqgrn', qg, k.astype(jnp.float32)) * d ** -0.5
    p = masked_softmax(s, mask[:, :, :, None, :])
    o = jnp.einsum('bqgrn,bqgnd->bqgrd', p, v.astype(jnp.float32))
    return o.reshape(B, Q, H, d)


def compress(rows, pe, w1, w2):
    B, L, G, d = rows.shape
    nc = (L - CMP_LEN) // CMP_STRIDE + 1
    npc = nc + 1
    pieces = rows[:, :npc * CMP_STRIDE].reshape(B, npc, CMP_STRIDE, G, d)
    w1 = w1.reshape(2, CMP_STRIDE, d, CMP_HIDDEN)
    pe = pe.reshape(2, CMP_STRIDE, 1, d)
    ha = jnp.einsum('bpsgd,sdh->bpgh', pieces + pe[0], w1[0])
    hb = jnp.einsum('bpsgd,sdh->bpgh', pieces + pe[1], w1[1])
    h = jax.nn.silu(ha[:, :nc] + hb[:, 1:nc + 1])
    out = jnp.einsum('bcgh,he->bcge', h, w2)
    cend = jnp.arange(nc, dtype=jnp.int32) * CMP_STRIDE + (CMP_LEN - 1)
    return out, cend


def select_blocks(p_cmp, qpos, seq_len):
    pg = p_cmp.sum(2).transpose(0, 2, 1, 3)
    nc = pg.shape[-1]
    nbs = -(-seq_len // SEL_LEN)
    ppb = SEL_LEN // CMP_STRIDE
    pp = jnp.pad(pg, ((0, 0), (0, 0), (0, 0), (0, nbs * ppb - nc)))
    piece = pp + jnp.pad(pp[..., :-1], ((0, 0), (0, 0), (0, 0), (1, 0)))
    blk = piece.reshape(pg.shape[0], pg.shape[1], pg.shape[2], nbs, ppb).sum(-1)
    j = jnp.arange(nbs, dtype=jnp.int32)[None, :]
    cur = (qpos // SEL_LEN)[:, None]
    forced = (j == 0) | (j == cur) | (j == cur - 1)
    valid = j * SEL_LEN <= qpos[:, None]
    score = jnp.where(forced[None, :, None, :], BIG, jnp.where(valid[None, :, None, :], blk, NEG))
    _, idx = lax.top_k(score, min(N_SEL, nbs))
    bvalid = idx * SEL_LEN <= qpos[None, :, None, None]
    return idx, bvalid


def nsa_attend(q, qpos, kc, vc, cend, seq_len, gather_sel, kw, vw, kwpos, gate):
    B, Q, H, d = q.shape
    o_cmp, p_cmp = attend_shared(q, kc, vc, cend[None, :] <= qpos[:, None])
    idx, bvalid = select_blocks(p_cmp, qpos, seq_len)
    tok = idx[..., None] * SEL_LEN + jnp.arange(SEL_LEN, dtype=jnp.int32)
    smask = bvalid[..., None] & (tok <= qpos[None, :, None, None, None])
    tok = tok.reshape(B, Q, N_KV_A, -1)
    ks, vs = gather_sel(tok)
    o_sel = attend_gathered(q, ks, vs, smask.reshape(B, Q, N_KV_A, -1))
    wmask = (kwpos[None, :] >= 0) & (kwpos[None, :] <= qpos[:, None]) & (qpos[:, None] - kwpos[None, :] < WINDOW)
    o_win, _ = attend_shared(q, kw, vw, wmask)
    g = jax.nn.sigmoid(gate.astype(jnp.float32)).reshape(B, Q, 3, H, 1)
    return g[:, :, 0] * o_cmp + g[:, :, 1] * o_sel + g[:, :, 2] * o_win


def retention(q, k, v, s0, chunk):
    B, T, H, dk = q.shape
    dv = v.shape[-1]
    nck = T // chunk
    log_g = jnp.log(1.0 - 2.0 ** (-5.0 - jnp.arange(H, dtype=jnp.float32)))
    i = jnp.arange(chunk, dtype=jnp.float32)
    diff = i[:, None] - i[None, :]
    dmat = jnp.where(diff >= 0, jnp.exp(log_g[:, None, None] * jnp.maximum(diff, 0.0)), 0.0)
    qdec = jnp.exp(log_g[None, :] * (i[:, None] + 1.0))[None, :, :, None]
    kdec = jnp.exp(log_g[None, :] * (chunk - 1.0 - i[:, None]))[None, :, :, None]
    tdec = jnp.exp(log_g * chunk)[None, :, None, None]

    def chunks(t):
        return t.astype(jnp.float32).reshape(B, nck, chunk, H, t.shape[-1]).transpose(1, 0, 2, 3, 4)

    def step(S, xs):
        qc, kc, vc = xs
        att = jnp.einsum('bihd,bjhd->bhij', qc, kc) * dmat[None]
        o = jnp.einsum('bhij,bjhe->bihe', att, vc) + jnp.einsum('bihd,bhde->bihe', qc * qdec, S)
        S = S * tdec + jnp.einsum('bjhd,bjhe->bhde', kc * kdec, vc)
        return S, o

    S, o = lax.scan(step, s0.astype(jnp.float32), (chunks(q), chunks(k), chunks(v)))
    return o.transpose(1, 0, 2, 3, 4).reshape(B, T, H, dv), S


def mixer_inputs(x, pos, norm_g, w_in, q_norm_g, k_sel_norm_g, k_win_norm_g):
    B, T, _ = x.shape
    z = rmsnorm(x, norm_g) @ w_in
    offs = [int(o) for o in np.cumsum(SPLIT_WIDTHS)[:-1]]
    (qa, kc, vc, ks, vs, kw, vw, ga, za, qr, kr, vr, zr, mg) = jnp.split(z, offs, axis=-1)
    kvh = lambda t: t.reshape(B, T, N_KV_A, HEAD_DIM)
    qa = rope(rmsnorm(qa.reshape(B, T, N_HEADS_A, HEAD_DIM), q_norm_g), pos)
    ks = rope(rmsnorm(kvh(ks), k_sel_norm_g), pos)
    kw = rope(rmsnorm(kvh(kw), k_win_norm_g), pos)
    qr = rope(qr.reshape(B, T, N_HEADS_R, DK_R), pos)
    kr = rope(kr.reshape(B, T, N_HEADS_R, DK_R), pos) * DK_R ** -0.5
    vr = vr.reshape(B, T, N_HEADS_R, DV_R)
    return (qa, kvh(kc), kvh(vc), ks, kvh(vs), kw, kvh(vw), ga, za, qr, kr, vr, zr, mg)


def compressed_kv(kc_rows, vc_rows, k_cmp_norm_g, cmp_k_pe, cmp_k_w1, cmp_k_w2, cmp_v_pe, cmp_v_w1, cmp_v_w2):
    kc, cend = compress(kc_rows, cmp_k_pe, cmp_k_w1, cmp_k_w2)
    kc = rope(rmsnorm(kc, k_cmp_norm_g), cend)
    vc, _ = compress(vc_rows, cmp_v_pe, cmp_v_w1, cmp_v_w2)
    return kc, vc, cend


def merge_out(x, o_a, z_a, o_r, z_r, mg, ret_gn_g, w_up_a, w_up_r, w_out):
    B, T, _ = x.shape
    ya = (o_a.reshape(B, T, Q_A_W) * jax.nn.silu(z_a.astype(jnp.float32))).astype(x.dtype) @ w_up_a
    mu = jnp.mean(o_r, axis=-1, keepdims=True)
    var = jnp.mean((o_r - mu) ** 2, axis=-1, keepdims=True)
    on = ((o_r - mu) * lax.rsqrt(var + EPS)).reshape(B, T, V_R_W) * ret_gn_g.astype(jnp.float32)
    yr = (on * jax.nn.silu(z_r.astype(jnp.float32))).astype(x.dtype) @ w_up_r
    ga, gr = jnp.split(mg, 2, axis=-1)
    m = jax.nn.sigmoid(ga.astype(jnp.float32)) * ya.astype(jnp.float32) + jax.nn.sigmoid(gr.astype(jnp.float32)) * yr.astype(jnp.float32)
    return x + m.astype(x.dtype) @ w_out


def setup_inputs(seed: int = 0) -> dict:
    key = jax.random.key(seed)
    ks = jax.random.split(key, 32)
    n_pages = PAST_LEN // PAGE_SIZE
    n_phys = (DEC_BATCH * n_pages * 5) // 4
    wb = min(WINDOW, PAST_LEN)
    nrm = lambda k, shape, scale: jax.random.normal(k, shape, jnp.float32) * scale
    pool = (n_phys, PAGE_SIZE, N_KV_A, HEAD_DIM)
    page_table = jax.random.permutation(ks[10], n_phys)[:DEC_BATCH * n_pages].reshape(DEC_BATCH, n_pages).astype(jnp.int32)
    return {
        "x_prompt": nrm(ks[0], (BATCH, SEQ, D_MODEL), 1.0),
        "x_sample": nrm(ks[1], (DEC_BATCH, DEC_SEQ, D_MODEL), 1.0),
        "cache_k_cmp": nrm(ks[2], pool, 1.0),
        "cache_v_cmp": nrm(ks[3], pool, 1.0),
        "cache_k_sel": nrm(ks[4], pool, 1.0),
        "cache_v_sel": nrm(ks[5], pool, 1.0),
        "cache_k_win": nrm(ks[6], (DEC_BATCH, wb, N_KV_A, HEAD_DIM), 1.0),
        "cache_v_win": nrm(ks[7], (DEC_BATCH, wb, N_KV_A, HEAD_DIM), 1.0),
        "state_ret": nrm(ks[8], (DEC_BATCH, N_HEADS_R, DK_R, DV_R), 0.1),
        "page_table": page_table,
        "norm_g": 1.0 + nrm(ks[11], (D_MODEL,), 0.01),
        "w_in": nrm(ks[12], (D_MODEL, D_IN), D_MODEL ** -0.5),
        "q_norm_g": 1.0 + nrm(ks[13], (HEAD_DIM,), 0.01),
        "k_cmp_norm_g": 1.0 + nrm(ks[14], (HEAD_DIM,), 0.01),
        "k_sel_norm_g": 1.0 + nrm(ks[15], (HEAD_DIM,), 0.01),
        "k_win_norm_g": 1.0 + nrm(ks[16], (HEAD_DIM,), 0.01),
        "cmp_k_pe": nrm(ks[17], (CMP_LEN, HEAD_DIM), 0.1),
        "cmp_k_w1": nrm(ks[18], (CMP_LEN * HEAD_DIM, CMP_HIDDEN), (CMP_LEN * HEAD_DIM) ** -0.5),
        "cmp_k_w2": nrm(ks[19], (CMP_HIDDEN, HEAD_DIM), CMP_HIDDEN ** -0.5),
        "cmp_v_pe": nrm(ks[20], (CMP_LEN, HEAD_DIM), 0.1),
        "cmp_v_w1": nrm(ks[21], (CMP_LEN * HEAD_DIM, CMP_HIDDEN), (CMP_LEN * HEAD_DIM) ** -0.5),
        "cmp_v_w2": nrm(ks[22], (CMP_HIDDEN, HEAD_DIM), CMP_HIDDEN ** -0.5),
        "ret_gn_g": 1.0 + nrm(ks[23], (V_R_W,), 0.01),
        "w_up_a": nrm(ks[24], (Q_A_W, D_MODEL), Q_A_W ** -0.5),
        "w_up_r": nrm(ks[25], (V_R_W, D_MODEL), V_R_W ** -0.5),
        "w_out": nrm(ks[26], (D_MODEL, D_MODEL), D_MODEL ** -0.5),
    }


def reference(x_prompt, x_sample, cache_k_cmp, cache_v_cmp, cache_k_sel, cache_v_sel, cache_k_win, cache_v_win,
              state_ret, page_table, norm_g, w_in, q_norm_g, k_cmp_norm_g, k_sel_norm_g, k_win_norm_g,
              cmp_k_pe, cmp_k_w1, cmp_k_w2, cmp_v_pe, cmp_v_w1, cmp_v_w2, ret_gn_g, w_up_a, w_up_r, w_out):
    cmp_params = (k_cmp_norm_g, cmp_k_pe, cmp_k_w1, cmp_k_w2, cmp_v_pe, cmp_v_w1, cmp_v_w2)
    gidx = jnp.arange(N_KV_A)[None, None, :, None]

    B, T, _ = x_prompt.shape
    pos_p = jnp.arange(T, dtype=jnp.int32)
    (qa, kc_rows, vc_rows, k_sel, v_sel, k_win, v_win, ga, za, qr, kr, vr, zr, mg) = mixer_inputs(
        x_prompt, pos_p, norm_g, w_in, q_norm_g, k_sel_norm_g, k_win_norm_g)
    kc, vc, cend = compressed_kv(kc_rows, vc_rows, *cmp_params)
    kw_pad = jnp.pad(k_win, ((0, 0), (WINDOW, 0), (0, 0), (0, 0)))
    vw_pad = jnp.pad(v_win, ((0, 0), (WINDOW, 0), (0, 0), (0, 0)))
    bidx_p = jnp.arange(B)[:, None, None, None]

    def gather_prompt(tok):
        tc = jnp.minimum(tok, T - 1)
        return k_sel[bidx_p, tc, gidx], v_sel[bidx_p, tc, gidx]

    def query_block(i):
        start = i * Q_BLOCK
        qb = lax.dynamic_slice_in_dim(qa, start, Q_BLOCK, axis=1)
        gb = lax.dynamic_slice_in_dim(ga, start, Q_BLOCK, axis=1)
        kwb = lax.dynamic_slice_in_dim(kw_pad, start, WINDOW + Q_BLOCK, axis=1)
        vwb = lax.dynamic_slice_in_dim(vw_pad, start, WINDOW + Q_BLOCK, axis=1)
        qpos = start + jnp.arange(Q_BLOCK, dtype=jnp.int32)
        kwpos = start - WINDOW + jnp.arange(WINDOW + Q_BLOCK, dtype=jnp.int32)
        return nsa_attend(qb, qpos, kc, vc, cend, T, gather_prompt, kwb, vwb, kwpos, gb)

    o_blocks = lax.map(query_block, jnp.arange(T // Q_BLOCK, dtype=jnp.int32))
    o_a = o_blocks.transpose(1, 0, 2, 3, 4).reshape(B, T, N_HEADS_A, HEAD_DIM)
    s0 = jnp.zeros((B, N_HEADS_R, DK_R, DV_R), jnp.float32)
    o_r, p_ret = retention(qr, kr, vr, s0, RET_CHUNK if T % RET_CHUNK == 0 else T)
    y_prompt = merge_out(x_prompt, o_a, za, o_r, zr, mg, ret_gn_g, w_up_a, w_up_r, w_out)
    wbp = min(WINDOW, T)
    p_k_win = k_win[:, T - wbp:]
    p_v_win = v_win[:, T - wbp:]
    p_k_cmp, p_v_cmp, p_k_sel, p_v_sel = kc_rows, vc_rows, k_sel, v_sel

    DB, S, _ = x_sample.shape
    psz = cache_k_cmp.shape[1]
    past_len = page_table.shape[1] * psz
    L = past_len + S
    pos_s = past_len + jnp.arange(S, dtype=jnp.int32)
    (qs, kc_new, vc_new, ks_new, vs_new, kw_new, vw_new, gs, zas, qrs, krs, vrs, zrs, mgs) = mixer_inputs(
        x_sample, pos_s, norm_g, w_in, q_norm_g, k_sel_norm_g, k_win_norm_g)
    kc_all = jnp.concatenate([cache_k_cmp[page_table].reshape(DB, past_len, N_KV_A, HEAD_DIM), kc_new], axis=1)
    vc_all = jnp.concatenate([cache_v_cmp[page_table].reshape(DB, past_len, N_KV_A, HEAD_DIM), vc_new], axis=1)
    kcs, vcs, cends = compressed_kv(kc_all, vc_all, *cmp_params)
    bidx_s = jnp.arange(DB)[:, None, None, None]

    def gather_sample(tok):
        pp = jnp.minimum(tok, past_len - 1)
        phys = page_table[bidx_s, pp // psz]
        off = pp % psz
        npos = jnp.clip(tok - past_len, 0, S - 1)
        in_past = (tok < past_len)[..., None]
        k = jnp.where(in_past, cache_k_sel[phys, off, gidx], ks_new[bidx_s, npos, gidx])
        v = jnp.where(in_past, cache_v_sel[phys, off, gidx], vs_new[bidx_s, npos, gidx])
        return k, v

    lb = cache_k_win.shape[1]
    kw_s = jnp.concatenate([cache_k_win, kw_new], axis=1)
    vw_s = jnp.concatenate([cache_v_win, vw_new], axis=1)
    kwpos_s = past_len - lb + jnp.arange(lb + S, dtype=jnp.int32)
    o_as = nsa_attend(qs, pos_s, kcs, vcs, cends, L, gather_sample, kw_s, vw_s, kwpos_s, gs)
    o_rs, s_ret = retention(qrs, krs, vrs, state_ret, RET_CHUNK if S % RET_CHUNK == 0 else S)
    y_sample = merge_out(x_sample, o_as, zas, o_rs, zrs, mgs, ret_gn_g, w_up_a, w_up_r, w_out)
    s_k_win = kw_s[:, S:]
    s_v_win = vw_s[:, S:]

    return (y_prompt, y_sample,
            p_k_cmp, p_v_cmp, p_k_sel, p_v_sel, p_k_win, p_v_win, p_ret.astype(x_prompt.dtype),
            kc_new, vc_new, ks_new, vs_new, s_k_win, s_v_win, s_ret.astype(state_ret.dtype))
```

```python
import functools

import numpy as np
import jax
import jax.numpy as jnp
from jax import lax
from jax.experimental import pallas as pl
from jax.experimental.pallas import tpu as pltpu

F32 = jnp.float32
BF16 = jnp.bfloat16

D_MODEL = 2048
HEAD_DIM = 128
N_HEADS_A = 8
N_KV_A = 2
GROUP_R = N_HEADS_A // N_KV_A
CMP_STRIDE = 16
CMP_LEN = 2 * CMP_STRIDE
SEL_LEN = 64
N_SEL = 16
WINDOW = 512
Q_BLOCK = 128
N_HEADS_R = 8
DK_R = 128
DV_R = 256
RET_CHUNK = 128
ROPE_THETA = 10000.0
EPS = 1e-6
NEG = -1e30
BIG = 1e30

Q_A_W = N_HEADS_A * HEAD_DIM
KV_A_W = N_KV_A * HEAD_DIM
GATE_A_W = 3 * N_HEADS_A
Q_R_W = N_HEADS_R * DK_R
V_R_W = N_HEADS_R * DV_R
MERGE_W = 2 * D_MODEL

LANES = 128
VMEM_LIMIT_BYTES = 56 * 1024 * 1024

_REF_SPLITS = (("qa", Q_A_W), ("kc", KV_A_W), ("vc", KV_A_W), ("ks", KV_A_W), ("vs", KV_A_W),
               ("kw", KV_A_W), ("vw", KV_A_W), ("ga", GATE_A_W), ("za", Q_A_W), ("qr", Q_R_W),
               ("kr", Q_R_W), ("vr", V_R_W), ("zr", V_R_W), ("mg", MERGE_W))
_REF_OFF = {}
_o = 0
for _n, _w in _REF_SPLITS:
    _REF_OFF[_n] = (_o, _w)
    _o += _w
D_IN = _o

_K_SPLITS = (("mg", MERGE_W), ("vr", V_R_W), ("zr", V_R_W), ("qa", Q_A_W), ("qr", Q_R_W),
             ("kr", Q_R_W), ("za", Q_A_W), ("ks", KV_A_W), ("kw", KV_A_W), ("kc", KV_A_W),
             ("vc", KV_A_W), ("vs", KV_A_W), ("vw", KV_A_W), ("ga", N_KV_A * LANES), ("pad", 256))
_COLS = {}
_o = 0
for _n, _w in _K_SPLITS:
    _COLS[_n] = _o
    _o += _w
D_INK = _o
PROJ_TN = 512
assert D_INK % PROJ_TN == 0
N_PROJ_TILES = D_INK // PROJ_TN


def _tile_kind(j):
    c = j * PROJ_TN
    if _COLS["qa"] <= c < _COLS["qa"] + Q_A_W or _COLS["ks"] <= c < _COLS["ks"] + 2 * KV_A_W:
        return 1
    if _COLS["qr"] <= c < _COLS["qr"] + 2 * Q_R_W:
        return 2
    return 0


_KIND1_TILES = tuple(j for j in range(N_PROJ_TILES) if _tile_kind(j) == 1)
_KIND2_TILES = tuple(j for j in range(N_PROJ_TILES) if _tile_kind(j) == 2)


def _cparams(sem):
    return pltpu.CompilerParams(dimension_semantics=sem, vmem_limit_bytes=VMEM_LIMIT_BYTES)


def _is_any(j, tiles):
    r = j == tiles[0]
    for t in tiles[1:]:
        r = r | (j == t)
    return r


def _rope_apply(y, cos, sin_signed):
    return y * cos + pltpu.roll(y, HEAD_DIM // 2, axis=1) * sin_signed


def _proj_kernel(x_ref, ng_ref, w_ref, mult_ref, cos_ref, sin_ref, o_ref, xn_ref):
    j = pl.program_id(1)

    @pl.when(j == 0)
    def _():
        x = x_ref[...]
        ms = jnp.mean(x * x, axis=-1, keepdims=True)
        xn_ref[...] = (x * lax.rsqrt(ms + EPS) * ng_ref[...]).astype(BF16)

    acc = jnp.dot(xn_ref[...], w_ref[...], preferred_element_type=F32)
    is1 = _is_any(j, _KIND1_TILES)
    is2 = _is_any(j, _KIND2_TILES)

    @pl.when(jnp.logical_not(is1 | is2))
    def _():
        o_ref[...] = acc

    @pl.when(is1)
    def _():
        cos = cos_ref[...]
        sin = sin_ref[...]
        for h in range(PROJ_TN // HEAD_DIM):
            sl = slice(h * HEAD_DIM, (h + 1) * HEAD_DIM)
            a = acc[:, sl]
            ms = jnp.mean(a * a, axis=-1, keepdims=True)
            y = a * lax.rsqrt(ms + EPS) * mult_ref[0, :, sl]
            o_ref[:, sl] = _rope_apply(y, cos, sin)

    @pl.when(is2)
    def _():
        cos = cos_ref[...]
        sin = sin_ref[...]
        for h in range(PROJ_TN // HEAD_DIM):
            sl = slice(h * HEAD_DIM, (h + 1) * HEAD_DIM)
            o_ref[:, sl] = _rope_apply(acc[:, sl], cos, sin) * mult_ref[0, :, sl]


def _project(x2d, norm_g, w_k, mult, cos_t, sin_t, tm):
    m = x2d.shape[0]
    n_pos_blocks = cos_t.shape[0] // tm
    return pl.pallas_call(
        _proj_kernel,
        grid=(m // tm, N_PROJ_TILES),
        in_specs=[
            pl.BlockSpec((tm, D_MODEL), lambda i, j: (i, 0)),
            pl.BlockSpec((1, D_MODEL), lambda i, j: (0, 0)),
            pl.BlockSpec((D_MODEL, PROJ_TN), lambda i, j: (0, j)),
            pl.BlockSpec((1, 1, PROJ_TN), lambda i, j: (j, 0, 0)),
            pl.BlockSpec((tm, HEAD_DIM), lambda i, j: (i % n_pos_blocks, 0)),
            pl.BlockSpec((tm, HEAD_DIM), lambda i, j: (i % n_pos_blocks, 0)),
        ],
        out_specs=pl.BlockSpec((tm, PROJ_TN), lambda i, j: (i, j)),
        out_shape=jax.ShapeDtypeStruct((m, D_INK), F32),
        scratch_shapes=[pltpu.VMEM((tm, D_MODEL), BF16)],
        compiler_params=_cparams(("arbitrary", "arbitrary")),
        name="proj",
    )(x2d, norm_g.reshape(1, D_MODEL), w_k, mult, cos_t, sin_t)


def _rope_tables(pos):
    half = HEAD_DIM // 2
    inv = ROPE_THETA ** (-jnp.arange(half, dtype=F32) / half)
    ang = pos.astype(F32)[:, None] * inv[None, :]
    cos = jnp.cos(ang)
    sin = jnp.sin(ang)
    return jnp.concatenate([cos, cos], axis=-1), jnp.concatenate([-sin, sin], axis=-1)


def _prep_w_in(w_in, q_norm_g, k_sel_norm_g, k_win_norm_g):
    def ref_cols(name):
        o, w = _REF_OFF[name]
        return w_in[:, o:o + w]

    ga = ref_cols("ga")
    ga_parts = []
    for g in range(N_KV_A):
        cols = [ga[:, br * N_HEADS_A + g * GROUP_R + hl:br * N_HEADS_A + g * GROUP_R + hl + 1]
                for br in range(3) for hl in range(GROUP_R)]
        ga_parts.append(jnp.concatenate(cols + [jnp.zeros((D_MODEL, LANES - 3 * GROUP_R), w_in.dtype)], axis=1))
    parts = []
    for name, w in _K_SPLITS:
        if name == "ga":
            parts.extend(ga_parts)
        elif name == "pad":
            parts.append(jnp.zeros((D_MODEL, w), w_in.dtype))
        else:
            parts.append(ref_cols(name))
    w_k = jnp.concatenate(parts, axis=1).astype(BF16)

    mult = jnp.ones((D_INK,), F32)
    mult = mult.at[_COLS["qa"]:_COLS["qa"] + Q_A_W].set(jnp.tile(q_norm_g.astype(F32), N_HEADS_A))
    mult = mult.at[_COLS["ks"]:_COLS["ks"] + KV_A_W].set(jnp.tile(k_sel_norm_g.astype(F32), N_KV_A))
    mult = mult.at[_COLS["kw"]:_COLS["kw"] + KV_A_W].set(jnp.tile(k_win_norm_g.astype(F32), N_KV_A))
    mult = mult.at[_COLS["kr"]:_COLS["kr"] + Q_R_W].set(DK_R ** -0.5)
    return w_k, mult.reshape(N_PROJ_TILES, 1, PROJ_TN)


PIECES_PER_PAGE_STEP = 16


def _prep_cmp_w1(w1):
    w = w1.reshape(2, CMP_STRIDE, HEAD_DIM, HEAD_DIM)
    w = jnp.transpose(w, (1, 2, 0, 3)).reshape(CMP_STRIDE, HEAD_DIM, 2 * HEAD_DIM)
    return w.reshape(CMP_STRIDE // 2, 2 * HEAD_DIM, 2 * HEAD_DIM).astype(BF16)


def _cmp_bias(pe_ref, w1_ref):
    acc = jnp.zeros((8, 2 * HEAD_DIM), F32)
    for s2 in range(CMP_STRIDE // 2):
        rows = []
        for s in (2 * s2, 2 * s2 + 1):
            r = pe_ref[pl.ds(s, 2, stride=CMP_STRIDE), :]
            rows.append(jnp.concatenate([r, jnp.zeros((6, HEAD_DIM), F32)], axis=0))
        a = jnp.concatenate(rows, axis=1).astype(BF16)
        acc = acc + jnp.dot(a, w1_ref[s2], preferred_element_type=F32)
    return acc[0:1, :HEAD_DIM] + acc[1:2, HEAD_DIM:]


def _cmp_finalize(h_ref, bias, w2_ref, g_ref, cos_ref, sin_ref, o_ref, npc, is_k):
    for g in range(N_KV_A):
        hg = h_ref[g]
        first = hg[:, :HEAD_DIM]
        second = pltpu.roll(hg[:, HEAD_DIM:], npc - 1, axis=0)
        pre = first + second + bias
        act = pre * jax.nn.sigmoid(pre)
        out = jnp.dot(act.astype(BF16), w2_ref[...], preferred_element_type=F32)
        if is_k:
            ms = jnp.mean(out * out, axis=-1, keepdims=True)
            out = _rope_apply(out * lax.rsqrt(ms + EPS) * g_ref[...], cos_ref[...], sin_ref[...])
        o_ref[0, :, g * HEAD_DIM:(g + 1) * HEAD_DIM] = out


def _cmp_prompt_kernel(rows0_ref, rows1_ref, pe_ref, w1_ref, w2_ref, g_ref, cos_ref, sin_ref, o_ref, h_ref,
                       *, npc, is_k):
    for g, rows_ref in enumerate((rows0_ref, rows1_ref)):
        acc = jnp.zeros((npc, 2 * HEAD_DIM), F32)
        for s2 in range(CMP_STRIDE // 2):
            a = jnp.concatenate(
                [rows_ref[pl.ds(s, npc, stride=CMP_STRIDE), :] for s in (2 * s2, 2 * s2 + 1)],
                axis=1).astype(BF16)
            acc = acc + jnp.dot(a, w1_ref[s2], preferred_element_type=F32)
        h_ref[g] = acc
    _cmp_finalize(h_ref, _cmp_bias(pe_ref, w1_ref), w2_ref, g_ref, cos_ref, sin_ref, o_ref, npc, is_k)


def _cmp_tables(npc):
    cend = jnp.arange(npc, dtype=jnp.int32) * CMP_STRIDE + (CMP_LEN - 1)
    return _rope_tables(cend)


def _compress_prompt(z, col, t, pe, w1k, w2, norm_g, is_k):
    b = z.shape[0] // t
    npc = t // CMP_STRIDE
    cos_t, sin_t = _cmp_tables(npc)
    full = lambda shape: pl.BlockSpec(shape, lambda i: (0,) * len(shape))
    return pl.pallas_call(
        functools.partial(_cmp_prompt_kernel, npc=npc, is_k=is_k),
        grid=(b,),
        in_specs=[
            pl.BlockSpec((t, HEAD_DIM), lambda i: (i, col // HEAD_DIM)),
            pl.BlockSpec((t, HEAD_DIM), lambda i: (i, col // HEAD_DIM + 1)),
            full((CMP_LEN, HEAD_DIM)),
            full((CMP_STRIDE // 2, 2 * HEAD_DIM, 2 * HEAD_DIM)),
            full((HEAD_DIM, HEAD_DIM)),
            full((1, HEAD_DIM)),
            full((npc, HEAD_DIM)),
            full((npc, HEAD_DIM)),
        ],
        out_specs=pl.BlockSpec((1, npc, KV_A_W), lambda i: (i, 0, 0)),
        out_shape=jax.ShapeDtypeStruct((b, npc, KV_A_W), F32),
        scratch_shapes=[pltpu.VMEM((N_KV_A, npc, 2 * HEAD_DIM), F32)],
        compiler_params=_cparams(("arbitrary",)),
        name="cmp_prompt_k" if is_k else "cmp_prompt_v",
    )(z, z, pe, w1k, w2.astype(BF16), norm_g.reshape(1, HEAD_DIM), cos_t, sin_t)


def _cmp_paged_kernel(pt_ref, *refs, npc, is_k, tokens_per_page):
    n = PIECES_PER_PAGE_STEP
    pages = refs[:n]
    pe_ref, w1_ref, w2_ref, g_ref, cos_ref, sin_ref, o_ref, h_ref = refs[n:]
    step = pl.program_id(1)
    ppp = tokens_per_page // CMP_STRIDE
    rows_per_piece = CMP_STRIDE * N_KV_A
    for g in range(N_KV_A):
        acc = jnp.zeros((n * ppp, 2 * HEAD_DIM), F32)
        for s2 in range(CMP_STRIDE // 2):
            cols = []
            for s in (2 * s2, 2 * s2 + 1):
                cols.append(jnp.concatenate(
                    [pg[pl.ds(N_KV_A * s + g, ppp, stride=rows_per_piece), :] for pg in pages], axis=0))
            a = jnp.concatenate(cols, axis=1).astype(BF16)
            acc = acc + jnp.dot(a, w1_ref[s2], preferred_element_type=F32)
        h_ref[g, pl.ds(pl.multiple_of(step * (n * ppp), n * ppp), n * ppp), :] = acc

    @pl.when(step == pl.num_programs(1) - 1)
    def _():
        _cmp_finalize(h_ref, _cmp_bias(pe_ref, w1_ref), w2_ref, g_ref, cos_ref, sin_ref, o_ref, npc, is_k)


def _compress_paged(cache2d, page_table, tokens_per_page, pe, w1k, w2, norm_g, is_k):
    db, n_pages = page_table.shape
    n = PIECES_PER_PAGE_STEP
    assert n_pages % n == 0 and tokens_per_page % CMP_STRIDE == 0
    npc = n_pages * tokens_per_page // CMP_STRIDE
    rows_pp = tokens_per_page * N_KV_A
    cos_t, sin_t = _cmp_tables(npc)
    full = lambda shape: pl.BlockSpec(shape, lambda i, s, pt: (0,) * len(shape))

    def page_spec(k):
        return pl.BlockSpec((rows_pp, HEAD_DIM), lambda i, s, pt: (pt[i, s * n + k], 0))

    grid_spec = pltpu.PrefetchScalarGridSpec(
        num_scalar_prefetch=1,
        grid=(db, n_pages // n),
        in_specs=[page_spec(k) for k in range(n)] + [
            full((CMP_LEN, HEAD_DIM)),
            full((CMP_STRIDE // 2, 2 * HEAD_DIM, 2 * HEAD_DIM)),
            full((HEAD_DIM, HEAD_DIM)),
            full((1, HEAD_DIM)),
            full((npc, HEAD_DIM)),
            full((npc, HEAD_DIM)),
        ],
        out_specs=pl.BlockSpec((1, npc, KV_A_W), lambda i, s, pt: (i, 0, 0)),
        scratch_shapes=[pltpu.VMEM((N_KV_A, npc, 2 * HEAD_DIM), F32)],
    )
    return pl.pallas_call(
        functools.partial(_cmp_paged_kernel, npc=npc, is_k=is_k, tokens_per_page=tokens_per_page),
        grid_spec=grid_spec,
        out_shape=jax.ShapeDtypeStruct((db, npc, KV_A_W), F32),
        compiler_params=_cparams(("arbitrary", "arbitrary")),
        name="cmp_paged_k" if is_k else "cmp_paged_v",
    )(page_table, *([cache2d] * n), pe, w1k, w2.astype(BF16), norm_g.reshape(1, HEAD_DIM), cos_t, sin_t)


ATT_TK = 512
_NT = (((1,), (1,)), ((), ()))


def _iota(shape, dim):
    return lax.broadcasted_iota(jnp.int32, shape, dim)


def _masked_softmax(s, maskf, axis):
    s = jnp.where(maskf > 0, s, NEG)
    m = jnp.max(s, axis=axis, keepdims=True)
    p = jnp.exp(s - m) * maskf
    return p / jnp.maximum(jnp.sum(p, axis=axis, keepdims=True), 1e-30)


def _select_mask_t(score, nbs, qpos_r, n_sel):
    j_full = _iota(score.shape, 0)
    rank = jnp.zeros(score.shape, F32)
    for i in range(nbs):
        r = score[i:i + 1, :]
        ge = jnp.where(r >= score, 1.0, 0.0)
        gt = jnp.where(r > score, 1.0, 0.0)
        rank = rank + jnp.where(j_full > i, ge, gt)
    chosen = jnp.where(rank < n_sel, 1.0, 0.0)
    return jnp.where(j_full * SEL_LEN <= qpos_r, chosen, 0.0)


def _nsa_prompt_kernel(q_ref, kc_ref, vc_ref, ks_ref, vs_ref, kw_ref, vw_ref, gate_ref, o_ref,
                       ksb, vsb, kwb, vwb, pg_scr, *, npc, nbs):
    i = pl.program_id(2)
    start = i * Q_BLOCK
    nq = GROUP_R * Q_BLOCK
    sc = HEAD_DIM ** -0.5

    @pl.when(i == 0)
    def _():
        ksb[...] = ks_ref[...].astype(BF16)
        vsb[...] = vs_ref[...].astype(BF16)
        kwb[...] = kw_ref[...].astype(BF16)
        vwb[...] = vw_ref[...].astype(BF16)

    q4 = jnp.concatenate([q_ref[:, h * HEAD_DIM:(h + 1) * HEAD_DIM] for h in range(GROUP_R)],
                         axis=0).astype(BF16)
    qpos_c = start + _iota((Q_BLOCK, 1), 0)
    qpos_c4 = jnp.concatenate([qpos_c] * GROUP_R, axis=0)
    qpos_r = start + _iota((1, Q_BLOCK), 1)
    qpos_r4 = jnp.concatenate([qpos_r] * GROUP_R, axis=1)

    kc = kc_ref[0].astype(BF16)
    vc = vc_ref[0].astype(BF16)
    s = lax.dot_general(q4, kc, _NT, preferred_element_type=F32) * sc
    cend_r = _iota((1, npc), 1) * CMP_STRIDE + (CMP_LEN - 1)
    p = _masked_softmax(s, jnp.where(cend_r <= qpos_c4, 1.0, 0.0), 1)
    o_cmp = jnp.dot(p.astype(BF16), vc, preferred_element_type=F32)

    st = lax.dot_general(kc, q4, _NT, preferred_element_type=F32) * sc
    cend_c = _iota((npc, 1), 0) * CMP_STRIDE + (CMP_LEN - 1)
    pt = _masked_softmax(st, jnp.where(cend_c <= qpos_r4, 1.0, 0.0), 0)
    pg = pt[:, 0:Q_BLOCK]
    for h in range(1, GROUP_R):
        pg = pg + pt[:, h * Q_BLOCK:(h + 1) * Q_BLOCK]
    pg_scr[0:8, :] = jnp.zeros((8, Q_BLOCK), F32)
    pg_scr[8:8 + npc, :] = pg
    ppb = SEL_LEN // CMP_STRIDE
    pk = [pg_scr[pl.ds(8 + k, nbs, stride=ppb), :] for k in range(-1, ppb)]
    blk = pk[1] + pk[0]
    for k in range(1, ppb):
        blk = blk + (pk[k + 1] + pk[k])
    j_c = _iota((nbs, 1), 0)
    cur = qpos_r // SEL_LEN
    forced = jnp.where(j_c == 0, 1.0, 0.0) + jnp.where(j_c == cur, 1.0, 0.0) + jnp.where(j_c == cur - 1, 1.0, 0.0)
    score = jnp.where(forced > 0, BIG, jnp.where(j_c * SEL_LEN <= qpos_r, blk, NEG))
    sel_t = _select_mask_t(score, nbs, qpos_r, min(N_SEL, nbs))
    if nbs < LANES:
        sel_t = jnp.concatenate([sel_t, jnp.zeros((LANES - nbs, Q_BLOCK), F32)], axis=0)
    sel_q = sel_t.T.astype(BF16)

    blk_id = _iota((LANES, 1), 0)

    def chunk(c, carry):
        m, l, acc = carry
        off = pl.multiple_of(c * ATT_TK, ATT_TK)
        k = ksb[pl.ds(off, ATT_TK), :]
        v = vsb[pl.ds(off, ATT_TK), :]
        sk = lax.dot_general(q4, k, _NT, preferred_element_type=F32) * sc
        kpos_r = off + _iota((1, ATT_TK), 1)
        expand = jnp.where(kpos_r // SEL_LEN == blk_id, 1.0, 0.0).astype(BF16)
        mf = jnp.dot(sel_q, expand, preferred_element_type=F32)
        mf = jnp.where(kpos_r <= qpos_c, mf, 0.0)
        mf4 = jnp.concatenate([mf] * GROUP_R, axis=0)
        sk = jnp.where(mf4 > 0, sk, NEG)
        m_new = jnp.maximum(m, jnp.max(sk, axis=1, keepdims=True))
        alpha = jnp.exp(m - m_new)
        pk_ = jnp.exp(sk - m_new) * mf4
        l = alpha * l + jnp.sum(pk_, axis=1, keepdims=True)
        acc = alpha * acc + jnp.dot(pk_.astype(BF16), v, preferred_element_type=F32)
        return m_new, l, acc

    n_chunks = (start + Q_BLOCK + ATT_TK - 1) // ATT_TK
    m, l, acc = lax.fori_loop(0, n_chunks, chunk,
                              (jnp.full((nq, 1), NEG, F32), jnp.zeros((nq, 1), F32), jnp.zeros((nq, HEAD_DIM), F32)))
    o_sel = acc / jnp.maximum(l, 1e-30)

    wlen = WINDOW + Q_BLOCK
    ws = pl.multiple_of(jnp.maximum(start - WINDOW, 0), Q_BLOCK)
    kw = kwb[pl.ds(ws, wlen), :]
    vw = vwb[pl.ds(ws, wlen), :]
    sw = lax.dot_general(q4, kw, _NT, preferred_element_type=F32) * sc
    kpos_w = ws + _iota((1, wlen), 1)
    mw = jnp.where(kpos_w <= qpos_c4, jnp.where(qpos_c4 - kpos_w < WINDOW, 1.0, 0.0), 0.0)
    pw = _masked_softmax(sw, mw, 1)
    o_win = jnp.dot(pw.astype(BF16), vw, preferred_element_type=F32)

    sig = jax.nn.sigmoid(gate_ref[...])
    for h in range(GROUP_R):
        rows = slice(h * Q_BLOCK, (h + 1) * Q_BLOCK)
        g0 = sig[:, h:h + 1]
        g1 = sig[:, GROUP_R + h:GROUP_R + h + 1]
        g2 = sig[:, 2 * GROUP_R + h:2 * GROUP_R + h + 1]
        o_ref[:, h * HEAD_DIM:(h + 1) * HEAD_DIM] = g0 * o_cmp[rows] + g1 * o_sel[rows] + g2 * o_win[rows]


def _nsa_prompt(z, kcmp, vcmp, t):
    b = z.shape[0] // t
    nqb = t // Q_BLOCK
    npc = t // CMP_STRIDE
    nbs = t // SEL_LEN
    assert t % ATT_TK == 0 and t >= WINDOW + Q_BLOCK and nbs <= LANES
    zcol = lambda name, w: (lambda bb, g, i: (bb, _COLS[name] // w + g))
    zrow = lambda name, w: (lambda bb, g, i: (bb * nqb + i, _COLS[name] // w + g))
    return pl.pallas_call(
        functools.partial(_nsa_prompt_kernel, npc=npc, nbs=nbs),
        grid=(b, N_KV_A, nqb),
        in_specs=[
            pl.BlockSpec((Q_BLOCK, GROUP_R * HEAD_DIM), zrow("qa", GROUP_R * HEAD_DIM)),
            pl.BlockSpec((1, npc, HEAD_DIM), lambda bb, g, i: (bb, 0, g)),
            pl.BlockSpec((1, npc, HEAD_DIM), lambda bb, g, i: (bb, 0, g)),
            pl.BlockSpec((t, HEAD_DIM), zcol("ks", HEAD_DIM)),
            pl.BlockSpec((t, HEAD_DIM), zcol("vs", HEAD_DIM)),
            pl.BlockSpec((t, HEAD_DIM), zcol("kw", HEAD_DIM)),
            pl.BlockSpec((t, HEAD_DIM), zcol("vw", HEAD_DIM)),
            pl.BlockSpec((Q_BLOCK, LANES), zrow("ga", LANES)),
        ],
        out_specs=pl.BlockSpec((Q_BLOCK, GROUP_R * HEAD_DIM), lambda bb, g, i: (bb * nqb + i, g)),
        out_shape=jax.ShapeDtypeStruct((b * t, Q_A_W), F32),
        scratch_shapes=[pltpu.VMEM((t, HEAD_DIM), BF16)] * 4 + [pltpu.VMEM((8 + npc, Q_BLOCK), F32)],
        compiler_params=_cparams(("arbitrary", "arbitrary", "arbitrary")),
        name="nsa_prompt",
    )(z, kcmp, vcmp, z, z, z, z, z)


def _ret_tables(chunk):
    log_g = jnp.log(1.0 - 2.0 ** (-5.0 - jnp.arange(N_HEADS_R, dtype=F32)))
    i = jnp.arange(chunk, dtype=F32)
    diff = i[:, None] - i[None, :]
    dmat = jnp.where(diff >= 0, jnp.exp(log_g[:, None, None] * jnp.maximum(diff, 0.0)), 0.0)
    qdec = jnp.exp(log_g[:, None] * (i[None, :] + 1.0))
    kdec = jnp.exp(log_g[:, None] * (chunk - 1.0 - i[None, :]))
    tdec = jnp.exp(log_g * chunk)
    return dmat, qdec, kdec, tdec


def _ret_prompt_kernel(q_ref, k_ref, v_ref, dmat_ref, qdec_ref, kdec_ref, tdec_ref, o_ref, s_out_ref, s_ref):
    c = pl.program_id(1)

    @pl.when(c == 0)
    def _():
        s_ref[...] = jnp.zeros(s_ref.shape, F32)

    for h in range(N_HEADS_R):
        qh = q_ref[:, h * DK_R:(h + 1) * DK_R]
        kh = k_ref[:, h * DK_R:(h + 1) * DK_R]
        vh = v_ref[:, h * DV_R:(h + 1) * DV_R].astype(BF16)
        st = s_ref[h]
        att = lax.dot_general(qh.astype(BF16), kh.astype(BF16), _NT, preferred_element_type=F32) * dmat_ref[h]
        o = jnp.dot(att.astype(BF16), vh, preferred_element_type=F32)
        o = o + jnp.dot((qh * qdec_ref[h]).astype(BF16), st.astype(BF16), preferred_element_type=F32)
        o_ref[:, h * DV_R:(h + 1) * DV_R] = o
        kd_t = (kh * kdec_ref[h]).T.astype(BF16)
        s_ref[h] = st * tdec_ref[h] + jnp.dot(kd_t, vh, preferred_element_type=F32)

    @pl.when(c == pl.num_programs(1) - 1)
    def _():
        s_out_ref[0] = s_ref[...]


def _retention_prompt(z, t):
    b = z.shape[0] // t
    chunk = RET_CHUNK
    assert t % chunk == 0
    nck = t // chunk
    dmat, qdec, kdec, tdec = _ret_tables(chunk)
    qdec_b = jnp.broadcast_to(qdec[:, :, None], (N_HEADS_R, chunk, DK_R))
    kdec_b = jnp.broadcast_to(kdec[:, :, None], (N_HEADS_R, chunk, DK_R))
    tdec_b = jnp.broadcast_to(tdec[:, None, None], (N_HEADS_R, 1, DV_R))
    full = lambda shape: pl.BlockSpec(shape, lambda bb, c: (0,) * len(shape))
    return pl.pallas_call(
        _ret_prompt_kernel,
        grid=(b, nck),
        in_specs=[
            pl.BlockSpec((chunk, Q_R_W), lambda bb, c: (bb * nck + c, _COLS["qr"] // Q_R_W)),
            pl.BlockSpec((chunk, Q_R_W), lambda bb, c: (bb * nck + c, _COLS["kr"] // Q_R_W)),
            pl.BlockSpec((chunk, V_R_W), lambda bb, c: (bb * nck + c, _COLS["vr"] // V_R_W)),
            full((N_HEADS_R, chunk, chunk)),
            full((N_HEADS_R, chunk, DK_R)),
            full((N_HEADS_R, chunk, DK_R)),
            full((N_HEADS_R, 1, DV_R)),
        ],
        out_specs=[
            pl.BlockSpec((chunk, V_R_W), lambda bb, c: (bb * nck + c, 0)),
            pl.BlockSpec((1, N_HEADS_R, DK_R, DV_R), lambda bb, c: (bb, 0, 0, 0)),
        ],
        out_shape=[
            jax.ShapeDtypeStruct((b * t, V_R_W), F32),
            jax.ShapeDtypeStruct((b, N_HEADS_R, DK_R, DV_R), F32),
        ],
        scratch_shapes=[pltpu.VMEM((N_HEADS_R, DK_R, DV_R), F32)],
        compiler_params=_cparams(("arbitrary", "arbitrary")),
        name="ret_prompt",
    )(z, z, z, dmat, qdec_b, kdec_b, tdec_b)


MERGE_TN = 512


def _merge_up_kernel(oa_ref, za_ref, or_ref, zr_ref, gn_ref, wa_ref, wr_ref, ga_ref, gr_ref, m_ref,
                     acta_ref, actr_ref):
    j = pl.program_id(1)

    @pl.when(j == 0)
    def _():
        za = za_ref[...]
        acta_ref[...] = (oa_ref[...] * (za * jax.nn.sigmoid(za))).astype(BF16)
        for h in range(N_HEADS_R):
            sl = slice(h * DV_R, (h + 1) * DV_R)
            o = or_ref[:, sl]
            mu = jnp.mean(o, axis=-1, keepdims=True)
            d = o - mu
            var = jnp.mean(d * d, axis=-1, keepdims=True)
            zr = zr_ref[:, sl]
            on = d * lax.rsqrt(var + EPS) * gn_ref[:, sl]
            actr_ref[:, sl] = (on * (zr * jax.nn.sigmoid(zr))).astype(BF16)

    ya = jnp.dot(acta_ref[...], wa_ref[...], preferred_element_type=F32)
    yr = jnp.dot(actr_ref[...], wr_ref[...], preferred_element_type=F32)
    m_ref[...] = (jax.nn.sigmoid(ga_ref[...]) * ya + jax.nn.sigmoid(gr_ref[...]) * yr).astype(BF16)


def _merge_up(o_a, o_r, z, ret_gn_g, wa_b, wr_b, tm):
    m = z.shape[0]
    tn = MERGE_TN
    nj = D_MODEL // tn
    return pl.pallas_call(
        _merge_up_kernel,
        grid=(m // tm, nj),
        in_specs=[
            pl.BlockSpec((tm, Q_A_W), lambda i, j: (i, 0)),
            pl.BlockSpec((tm, Q_A_W), lambda i, j: (i, _COLS["za"] // Q_A_W)),
            pl.BlockSpec((tm, V_R_W), lambda i, j: (i, 0)),
            pl.BlockSpec((tm, V_R_W), lambda i, j: (i, _COLS["zr"] // V_R_W)),
            pl.BlockSpec((1, V_R_W), lambda i, j: (0, 0)),
            pl.BlockSpec((Q_A_W, tn), lambda i, j: (0, j)),
            pl.BlockSpec((V_R_W, tn), lambda i, j: (0, j)),
            pl.BlockSpec((tm, tn), lambda i, j: (i, _COLS["mg"] // tn + j)),
            pl.BlockSpec((tm, tn), lambda i, j: (i, _COLS["mg"] // tn + nj + j)),
        ],
        out_specs=pl.BlockSpec((tm, tn), lambda i, j: (i, j)),
        out_shape=jax.ShapeDtypeStruct((m, D_MODEL), BF16),
        scratch_shapes=[pltpu.VMEM((tm, Q_A_W), BF16), pltpu.VMEM((tm, V_R_W), BF16)],
        compiler_params=_cparams(("arbitrary", "arbitrary")),
        name="merge_up",
    )(o_a, z, o_r, z, ret_gn_g.reshape(1, V_R_W), wa_b, wr_b, z, z)


def _merge_out_kernel(m_ref, w_ref, x_ref, y_ref):
    y_ref[...] = x_ref[...] + jnp.dot(m_ref[...], w_ref[...], preferred_element_type=F32)


def _merge_out(mm, wo_b, x2d, tm):
    m = x2d.shape[0]
    tn = MERGE_TN
    return pl.pallas_call(
        _merge_out_kernel,
        grid=(m // tm, D_MODEL // tn),
        in_specs=[
            pl.BlockSpec((tm, D_MODEL), lambda i, j: (i, 0)),
            pl.BlockSpec((D_MODEL, tn), lambda i, j: (0, j)),
            pl.BlockSpec((tm, tn), lambda i, j: (i, j)),
        ],
        out_specs=pl.BlockSpec((tm, tn), lambda i, j: (i, j)),
        out_shape=jax.ShapeDtypeStruct((m, D_MODEL), F32),
        compiler_params=_cparams(("arbitrary", "arbitrary")),
        name="merge_out",
    )(mm, wo_b, x2d)


def _heads_to_rows(ref, n_heads, width):
    return jnp.concatenate([ref[0, :, h * width:(h + 1) * width] for h in range(n_heads)], axis=0)


def _row_to_col(row):
    n = row.shape[1]
    eye = _iota((n, n), 0) == _iota((n, n), 1)
    return jnp.sum(jnp.where(eye, jnp.broadcast_to(row, (n, n)), 0.0), axis=1, keepdims=True)


def _split3_bf16(x):
    hi = x.astype(BF16)
    r = x - hi.astype(F32)
    mid = r.astype(BF16)
    lo = (r - mid.astype(F32)).astype(BF16)
    return hi, mid, lo


def _sel_coeff(npc, nc, nbp):
    ppb = SEL_LEN // CMP_STRIDE
    k = np.arange(npc)[:, None]
    j = np.arange(nbp)[None, :]
    a = ((k // ppb) == j).astype(np.float32) + (((k + 1) // ppb) == j).astype(np.float32)
    a = a * (k < nc)
    return jnp.asarray(a, dtype=BF16)


def _nsa_sample_select_kernel(q_ref, kc_ref, vc_ref, coef_ref, ocmp_ref, idx_ref, *, npc, nbp, qpos, n_sel):
    sc = HEAD_DIM ** -0.5
    q8 = _heads_to_rows(q_ref, N_HEADS_A, HEAD_DIM).astype(BF16)
    row_g = _iota((N_HEADS_A, 1), 0) // GROUP_R
    cend_r = _iota((1, npc), 1) * CMP_STRIDE + (CMP_LEN - 1)
    cmask = jnp.broadcast_to(jnp.where(cend_r <= qpos, 1.0, 0.0), (N_HEADS_A, npc))
    j_r = _iota((1, nbp), 1)
    cur = qpos // SEL_LEN
    forced = (jnp.where(j_r == 0, 1.0, 0.0) + jnp.where(j_r == cur, 1.0, 0.0)
              + jnp.where(j_r == cur - 1, 1.0, 0.0))
    o_cmp = jnp.zeros((N_HEADS_A, HEAD_DIM), F32)
    for g in range(N_KV_A):
        kc = kc_ref[0, :, g * HEAD_DIM:(g + 1) * HEAD_DIM].astype(BF16)
        vc = vc_ref[0, :, g * HEAD_DIM:(g + 1) * HEAD_DIM].astype(BF16)
        s = lax.dot_general(q8, kc, _NT, preferred_element_type=F32) * sc
        p = _masked_softmax(s, cmask, 1)
        o_g = jnp.dot(p.astype(BF16), vc, preferred_element_type=F32)
        o_cmp = jnp.where(row_g == g, o_g, o_cmp)
        pg = jnp.sum(jnp.where(row_g == g, p, 0.0), axis=0, keepdims=True)
        pg8 = jnp.broadcast_to(pg, (8, npc))
        blk = jnp.zeros((8, nbp), F32)
        for part in _split3_bf16(pg8):
            blk = blk + jnp.dot(part, coef_ref[...], preferred_element_type=F32)
        blk = blk[0:1, :]
        score = jnp.where(forced > 0, BIG, jnp.where(j_r * SEL_LEN <= qpos, blk, NEG))
        col = _row_to_col(score)
        i_lt_j = _iota((nbp, nbp), 0) < _iota((nbp, nbp), 1)
        ge = jnp.where(col >= score, 1.0, 0.0)
        gt = jnp.where(col > score, 1.0, 0.0)
        rank = jnp.sum(jnp.where(i_lt_j, ge, gt), axis=0, keepdims=True)
        slot = _iota((n_sel, 1), 0).astype(F32)
        hit = jnp.where(rank == slot, j_r.astype(F32), 0.0)
        idx = jnp.sum(hit, axis=1, keepdims=True).astype(jnp.int32)
        idx_ref[0, g] = jnp.broadcast_to(idx, (n_sel, LANES))
    ocmp_ref[0] = o_cmp


def _nsa_sample_select(zs3, kcs, vcs, past_len):
    db = zs3.shape[0]
    npc = kcs.shape[1]
    seq_len = past_len + 1
    nc = (seq_len - CMP_LEN) // CMP_STRIDE + 1
    nbs = -(-seq_len // SEL_LEN)
    nbp = -(-nbs // LANES) * LANES
    n_sel = min(N_SEL, nbs)
    assert npc >= nc
    coef = _sel_coeff(npc, nc, nbp)
    return pl.pallas_call(
        functools.partial(_nsa_sample_select_kernel, npc=npc, nbp=nbp, qpos=past_len, n_sel=n_sel),
        grid=(db,),
        in_specs=[
            pl.BlockSpec((1, 1, Q_A_W), lambda b: (b, 0, _COLS["qa"] // Q_A_W)),
            pl.BlockSpec((1, npc, KV_A_W), lambda b: (b, 0, 0)),
            pl.BlockSpec((1, npc, KV_A_W), lambda b: (b, 0, 0)),
            pl.BlockSpec((npc, nbp), lambda b: (0, 0)),
        ],
        out_specs=[
            pl.BlockSpec((1, N_HEADS_A, HEAD_DIM), lambda b: (b, 0, 0)),
            pl.BlockSpec((1, N_KV_A, n_sel, LANES), lambda b: (b, 0, 0, 0)),
        ],
        out_shape=[
            jax.ShapeDtypeStruct((db, N_HEADS_A, HEAD_DIM), F32),
            jax.ShapeDtypeStruct((db, N_KV_A, n_sel, LANES), jnp.int32),
        ],
        compiler_params=_cparams(("arbitrary",)),
        name="nsa_sample_select",
    )(zs3, kcs, vcs, coef)


def _attend_with_new(q8, q8b, k, v, maskf, k_new, v_new, new_valid):
    sc = HEAD_DIM ** -0.5
    s = lax.dot_general(q8b, k, _NT, preferred_element_type=F32) * sc
    s = jnp.where(maskf > 0, s, NEG)
    s_new = jnp.sum(q8 * k_new, axis=1, keepdims=True) * sc
    s_new = jnp.where(new_valid > 0, s_new, NEG)
    m = jnp.maximum(jnp.max(s, axis=1, keepdims=True), s_new)
    p = jnp.exp(s - m) * maskf
    p_new = jnp.exp(s_new - m) * new_valid
    l = jnp.sum(p, axis=1, keepdims=True) + p_new
    o = jnp.dot(p.astype(BF16), v, preferred_element_type=F32) + p_new * v_new
    return o / jnp.maximum(l, 1e-30)


def _nsa_sample_attend_kernel(idx_ref, pt_ref, q_ref, *refs, n_sel, past_len, win_len):
    nk = N_KV_A * n_sel
    k_refs = refs[:nk]
    v_refs = refs[nk:2 * nk]
    ksn_ref, vsn_ref, kwc_ref, vwc_ref, kwn_ref, vwn_ref, gate_ref, ocmp_ref, o_ref = refs[2 * nk:]
    b = pl.program_id(0)
    qpos = past_len
    q8 = _heads_to_rows(q_ref, N_HEADS_A, HEAD_DIM)
    q8b = q8.astype(BF16)
    row_g = _iota((N_HEADS_A, 1), 0) // GROUP_R
    nsk = n_sel * SEL_LEN
    lane = _iota((1, nsk), 1)
    o_sel = jnp.zeros((N_HEADS_A, HEAD_DIM), F32)
    o_win = jnp.zeros((N_HEADS_A, HEAD_DIM), F32)
    for g in range(N_KV_A):
        gl = slice(g * HEAD_DIM, (g + 1) * HEAD_DIM)
        k = jnp.concatenate([k_refs[g * n_sel + j][pl.ds(g, SEL_LEN, stride=N_KV_A), :] for j in range(n_sel)],
                            axis=0).astype(BF16)
        v = jnp.concatenate([v_refs[g * n_sel + j][pl.ds(g, SEL_LEN, stride=N_KV_A), :] for j in range(n_sel)],
                            axis=0).astype(BF16)
        tok = jnp.zeros((1, nsk), jnp.int32)
        new_valid = jnp.zeros((1, 1), F32)
        for j in range(n_sel):
            bj = idx_ref[b, g, j]
            tok = jnp.where(lane // SEL_LEN == j, bj * SEL_LEN + lane % SEL_LEN, tok)
            new_valid = new_valid + jnp.where(bj == qpos // SEL_LEN, 1.0, 0.0)
        maskf = jnp.where(tok < past_len, jnp.where(tok <= qpos, 1.0, 0.0), 0.0)
        maskf = jnp.broadcast_to(maskf, (N_HEADS_A, nsk))
        o_g = _attend_with_new(q8, q8b, k, v, maskf, ksn_ref[0, :, gl], vsn_ref[0, :, gl], new_valid)
        o_sel = jnp.where(row_g == g, o_g, o_sel)

        kw = kwc_ref[0, pl.ds(g, win_len, stride=N_KV_A), :].astype(BF16)
        vw = vwc_ref[0, pl.ds(g, win_len, stride=N_KV_A), :].astype(BF16)
        kpos = past_len - win_len + _iota((1, win_len), 1)
        mw = jnp.where(kpos >= 0, jnp.where(kpos <= qpos, jnp.where(qpos - kpos < WINDOW, 1.0, 0.0), 0.0), 0.0)
        mw = jnp.broadcast_to(mw, (N_HEADS_A, win_len))
        o_g = _attend_with_new(q8, q8b, kw, vw, mw, kwn_ref[0, :, gl], vwn_ref[0, :, gl], jnp.ones((1, 1), F32))
        o_win = jnp.where(row_g == g, o_g, o_win)

    sig = jnp.broadcast_to(jax.nn.sigmoid(gate_ref[0]), (N_HEADS_A, N_KV_A * LANES))
    glane = _iota((N_HEADS_A, N_KV_A * LANES), 1)
    row = _iota((N_HEADS_A, 1), 0)
    base = (row // GROUP_R) * LANES + row % GROUP_R
    gate = [jnp.sum(jnp.where(glane == base + br * GROUP_R, sig, 0.0), axis=1, keepdims=True) for br in range(3)]
    o = gate[0] * ocmp_ref[0] + gate[1] * o_sel + gate[2] * o_win
    for h in range(N_HEADS_A):
        o_ref[0, :, h * HEAD_DIM:(h + 1) * HEAD_DIM] = o[h:h + 1, :]


def _nsa_sample_attend(zs3, idx, page_table, ksel2d, vsel2d, kwc, vwc, o_cmp, past_len, tokens_per_page):
    db, n_pages = page_table.shape
    n_sel = idx.shape[2]
    spp = tokens_per_page // SEL_LEN
    win_len = kwc.shape[1] // N_KV_A
    rows = SEL_LEN * N_KV_A

    def blk_spec(g, j):
        def imap(b, idx_r, pt_r):
            blk = idx_r[b, g, j]
            page = jnp.minimum(blk // spp, n_pages - 1)
            return (pt_r[b, page] * spp + blk % spp, 0)
        return pl.BlockSpec((rows, HEAD_DIM), imap)

    zspec = lambda name, w: pl.BlockSpec((1, 1, w), lambda b, i_r, p_r: (b, 0, _COLS[name] // w))
    kv_specs = [blk_spec(g, j) for g in range(N_KV_A) for j in range(n_sel)]
    grid_spec = pltpu.PrefetchScalarGridSpec(
        num_scalar_prefetch=2,
        grid=(db,),
        in_specs=[zspec("qa", Q_A_W)] + kv_specs + kv_specs + [
            zspec("ks", KV_A_W), zspec("vs", KV_A_W),
            pl.BlockSpec((1, win_len * N_KV_A, HEAD_DIM), lambda b, i_r, p_r: (b, 0, 0)),
            pl.BlockSpec((1, win_len * N_KV_A, HEAD_DIM), lambda b, i_r, p_r: (b, 0, 0)),
            zspec("kw", KV_A_W), zspec("vw", KV_A_W), zspec("ga", N_KV_A * LANES),
            pl.BlockSpec((1, N_HEADS_A, HEAD_DIM), lambda b, i_r, p_r: (b, 0, 0)),
        ],
        out_specs=pl.BlockSpec((1, 1, Q_A_W), lambda b, i_r, p_r: (b, 0, 0)),
    )
    nk = N_KV_A * n_sel
    return pl.pallas_call(
        functools.partial(_nsa_sample_attend_kernel, n_sel=n_sel, past_len=past_len, win_len=win_len),
        grid_spec=grid_spec,
        out_shape=jax.ShapeDtypeStruct((db, 1, Q_A_W), F32),
        compiler_params=_cparams(("arbitrary",)),
        name="nsa_sample_attend",
    )(idx, page_table, zs3, *([ksel2d] * nk), *([vsel2d] * nk), zs3, zs3, kwc, vwc, zs3, zs3, zs3, o_cmp)


def _ret_sample_kernel(q_ref, k_ref, v_ref, tdec_ref, s_ref, o_ref, s_out_ref):
    q8 = _heads_to_rows(q_ref, N_HEADS_R, DK_R)
    k8 = _heads_to_rows(k_ref, N_HEADS_R, DK_R)
    for h in range(N_HEADS_R):
        qcol = _row_to_col(q8[h:h + 1, :])
        kcol = _row_to_col(k8[h:h + 1, :])
        vrow = v_ref[0, :, h * DV_R:(h + 1) * DV_R]
        s_new = s_ref[0, h] * tdec_ref[h] + kcol * vrow
        s_out_ref[0, h] = s_new
        o_ref[0, :, h * DV_R:(h + 1) * DV_R] = jnp.sum(qcol * s_new, axis=0, keepdims=True)


def _retention_sample(zs3, state):
    db = zs3.shape[0]
    _, _, _, tdec = _ret_tables(1)
    tdec_b = jnp.broadcast_to(tdec[:, None, None], (N_HEADS_R, 1, DV_R))
    return pl.pallas_call(
        _ret_sample_kernel,
        grid=(db,),
        in_specs=[
            pl.BlockSpec((1, 1, Q_R_W), lambda b: (b, 0, _COLS["qr"] // Q_R_W)),
            pl.BlockSpec((1, 1, Q_R_W), lambda b: (b, 0, _COLS["kr"] // Q_R_W)),
            pl.BlockSpec((1, 1, V_R_W), lambda b: (b, 0, _COLS["vr"] // V_R_W)),
            pl.BlockSpec((N_HEADS_R, 1, DV_R), lambda b: (0, 0, 0)),
            pl.BlockSpec((1, N_HEADS_R, DK_R, DV_R), lambda b: (b, 0, 0, 0)),
        ],
        out_specs=[
            pl.BlockSpec((1, 1, V_R_W), lambda b: (b, 0, 0)),
            pl.BlockSpec((1, N_HEADS_R, DK_R, DV_R), lambda b: (b, 0, 0, 0)),
        ],
        out_shape=[
            jax.ShapeDtypeStruct((db, 1, V_R_W), F32),
            jax.ShapeDtypeStruct(state.shape, F32),
        ],
        compiler_params=_cparams(("arbitrary",)),
        name="ret_sample",
    )(zs3, zs3, zs3, tdec_b, state)


def _row_tile(m, cap):
    tm = min(m, cap)
    assert m % tm == 0
    return tm


def kernel(x_prompt, x_sample, cache_k_cmp, cache_v_cmp, cache_k_sel, cache_v_sel, cache_k_win, cache_v_win,
           state_ret, page_table, norm_g, w_in, q_norm_g, k_cmp_norm_g, k_sel_norm_g, k_win_norm_g,
           cmp_k_pe, cmp_k_w1, cmp_k_w2, cmp_v_pe, cmp_v_w1, cmp_v_w2, ret_gn_g, w_up_a, w_up_r, w_out):
    b, t, _ = x_prompt.shape
    db, s_len, _ = x_sample.shape
    n_pages = page_table.shape[1]
    psz = cache_k_cmp.shape[1]
    past_len = n_pages * psz
    assert s_len == 1 and past_len % SEL_LEN == 0 and psz % SEL_LEN == 0

    w_k, mult = _prep_w_in(w_in, q_norm_g, k_sel_norm_g, k_win_norm_g)
    wa_b, wr_b, wo_b = w_up_a.astype(BF16), w_up_r.astype(BF16), w_out.astype(BF16)
    w1k, w1v = _prep_cmp_w1(cmp_k_w1), _prep_cmp_w1(cmp_v_w1)

    def kv_heads(z, name, lead):
        return z[:, _COLS[name]:_COLS[name] + KV_A_W].reshape(lead + (N_KV_A, HEAD_DIM))

    x2d = x_prompt.reshape(b * t, D_MODEL)
    tm = _row_tile(t, 1024)
    cos_p, sin_p = _rope_tables(jnp.arange(t, dtype=jnp.int32))
    z = _project(x2d, norm_g, w_k, mult, cos_p, sin_p, tm)
    kcmp = _compress_prompt(z, _COLS["kc"], t, cmp_k_pe, w1k, cmp_k_w2, k_cmp_norm_g, True)
    vcmp = _compress_prompt(z, _COLS["vc"], t, cmp_v_pe, w1v, cmp_v_w2, k_cmp_norm_g, False)
    o_a = _nsa_prompt(z, kcmp, vcmp, t)
    o_r, p_ret = _retention_prompt(z, t)
    mm = _merge_up(o_a, o_r, z, ret_gn_g, wa_b, wr_b, _row_tile(t, 512))
    y_prompt = _merge_out(mm, wo_b, x2d, tm).reshape(b, t, D_MODEL)
    p_k_cmp, p_v_cmp, p_k_sel, p_v_sel, p_k_win, p_v_win = (
        kv_heads(z, n, (b, t)) for n in ("kc", "vc", "ks", "vs", "kw", "vw"))
    wbp = min(WINDOW, t)
    p_k_win = p_k_win[:, t - wbp:]
    p_v_win = p_v_win[:, t - wbp:]

    xs2d = x_sample.reshape(db, D_MODEL)
    cos_s, sin_s = _rope_tables(jnp.full((db,), past_len, jnp.int32))
    zs = _project(xs2d, norm_g, w_k, mult, cos_s, sin_s, db)
    zs3 = zs.reshape(db, 1, D_INK)
    kcs = _compress_paged(cache_k_cmp.reshape(-1, HEAD_DIM), page_table, psz, cmp_k_pe, w1k, cmp_k_w2,
                          k_cmp_norm_g, True)
    vcs = _compress_paged(cache_v_cmp.reshape(-1, HEAD_DIM), page_table, psz, cmp_v_pe, w1v, cmp_v_w2,
                          k_cmp_norm_g, False)
    o_cmp_s, idx = _nsa_sample_select(zs3, kcs, vcs, past_len)
    lb = cache_k_win.shape[1]
    o_as = _nsa_sample_attend(zs3, idx[..., 0], page_table,
                              cache_k_sel.reshape(-1, HEAD_DIM), cache_v_sel.reshape(-1, HEAD_DIM),
                              cache_k_win.reshape(db, lb * N_KV_A, HEAD_DIM),
                              cache_v_win.reshape(db, lb * N_KV_A, HEAD_DIM), o_cmp_s, past_len, psz)
    o_rs, s_ret = _retention_sample(zs3, state_ret)
    mms = _merge_up(o_as.reshape(db, Q_A_W), o_rs.reshape(db, V_R_W), zs, ret_gn_g, wa_b, wr_b, db)
    y_sample = _merge_out(mms, wo_b, xs2d, db).reshape(db, 1, D_MODEL)
    s_k_cmp, s_v_cmp, s_k_sel, s_v_sel, kw_new, vw_new = (
        kv_heads(zs, n, (db, 1)) for n in ("kc", "vc", "ks", "vs", "kw", "vw"))
    s_k_win = jnp.concatenate([cache_k_win, kw_new], axis=1)[:, 1:]
    s_v_win = jnp.concatenate([cache_v_win, vw_new], axis=1)[:, 1:]

    return (y_prompt, y_sample, p_k_cmp, p_v_cmp, p_k_sel, p_v_sel, p_k_win, p_v_win, p_ret,
            s_k_cmp, s_v_cmp, s_k_sel, s_v_sel, s_k_win, s_v_win, s_ret)
```
